```python
import jax
import jax.numpy as jnp
from jax import lax
import numpy as np

D_MODEL = 4096
BATCH = 4
SEQ = 2048
DEPTH = 2
DEC_BATCH = 8
DEC_SEQ = 1
PAST_LEN = 16384
PAGE_SIZE = 128

N_MIXERS = 2
N_POOL_LAYERS = (DEPTH + 1) // 2
N_NSA_LAYERS = DEPTH // 2

POOL_WINDOWS = (2, 4, 8, 16)
POOL_GROUPS = 4
POOL_CG = D_MODEL // POOL_GROUPS
POOL_BUF = max(POOL_WINDOWS) - 1

HEAD_DIM = 128
N_HEADS = D_MODEL // HEAD_DIM
N_KV_HEADS = 4
GQA = N_HEADS // N_KV_HEADS
CMP_STRIDE = 16
CMP_BLOCK = 2 * CMP_STRIDE
CMP_HIDDEN = 128
SEL_BLOCK = 64
TOP_N = 16
WINDOW = 512
N_BRANCH = 3
NSA_Q_BLOCK = 32
ROPE_THETA = 10000.0
Q_COLS = N_HEADS * HEAD_DIM
PG_COLS = 4 * N_KV_HEADS * HEAD_DIM
WIN_COLS = 2 * N_KV_HEADS * HEAD_DIM
GATE_COLS = N_BRANCH * N_HEADS
IN_COLS = Q_COLS + PG_COLS + WIN_COLS + GATE_COLS
SLOT_KSEL = 2
SLOT_VSEL = 3
FORCE_SCORE = 1e4
NEG_INF = -1e30
TINY = 1e-30

N_EXPERTS = 64
N_EXPERT_GROUPS = 8
EXPERTS_PER_GROUP = N_EXPERTS // N_EXPERT_GROUPS
GROUP_SCORE_TOPK = 2
TOP_K = 2
EXPERT_FF = 1024
MOE_BLOCK = 128
MOE_BLOCK_SMALL = 8

NORM_EPS = 1e-6
ADA_INIT = 0.5

kernel_name = 'hybrid_pool_nsa_moe_adaln_step'


def _rms(x, g):
    xf = x.astype(jnp.float32)
    y = xf * lax.rsqrt(jnp.mean(xf * xf, axis=-1, keepdims=True) + NORM_EPS)
    return y.astype(x.dtype) * g


def _modulate(x, g, shift, scale):
    return _rms(x, g) * (1 + scale) + shift


def _ada(c, w, b):
    m = jax.nn.silu(c) @ w + b
    return jnp.split(m[:, None, :], 6, axis=-1)


def _rope(x, pos):
    half = HEAD_DIM // 2
    inv = jnp.power(ROPE_THETA, -jnp.arange(half, dtype=jnp.float32) / half)
    ang = pos.astype(jnp.float32)[:, None] * inv[None, :]
    cos = jnp.cos(ang)[None, :, None, :]
    sin = jnp.sin(ang)[None, :, None, :]
    xf = x.astype(jnp.float32)
    x1, x2 = xf[..., :half], xf[..., half:]
    return jnp.concatenate([x1 * cos - x2 * sin, x2 * cos + x1 * sin], axis=-1).astype(x.dtype)


def _masked_probs(s, mask):
    s = jnp.where(mask, s, NEG_INF)
    m = jnp.max(s, axis=-1, keepdims=True)
    p = jnp.where(mask, jnp.exp(s - m), 0.0)
    return p / jnp.maximum(jnp.sum(p, axis=-1, keepdims=True), TINY)


def _pool_mix(h, pos0, n_out, w, scale):
    B, L, _ = h.shape
    hf = h.astype(jnp.float32)
    cs = jnp.cumsum(hf, axis=1)
    pos = pos0 + jnp.arange(L)
    means = []
    for g, win in enumerate(POOL_WINDOWS):
        c = cs[..., g * POOL_CG:(g + 1) * POOL_CG]
        lag = jnp.concatenate([jnp.zeros_like(c[:, :win]), c[:, :L - win]], axis=1)
        cnt = jnp.minimum(pos + 1, win).astype(jnp.float32)
        means.append((c - lag) / cnt[None, :, None])
    pooled = (jnp.concatenate(means, axis=-1) - hf)[:, L - n_out:]
    pooled = pooled.reshape(B, n_out, POOL_GROUPS, POOL_CG).astype(h.dtype)
    y = jnp.einsum('blgc,gcd->blgd', pooled, w).reshape(B, n_out, D_MODEL)
    return y * scale


def _nsa_project(h, pos, w_in, qk_g):
    B, L, _ = h.shape
    proj = h @ w_in
    q, kv_pg, kv_win, g = jnp.split(proj, [Q_COLS, Q_COLS + PG_COLS, Q_COLS + PG_COLS + WIN_COLS], axis=-1)
    qn = _rms(q.reshape(B, L, N_HEADS, HEAD_DIM), qk_g[0])
    q_raw = qn.reshape(B, L, N_KV_HEADS, GQA, HEAD_DIM)
    q_rot = _rope(qn, pos).reshape(B, L, N_KV_HEADS, GQA, HEAD_DIM)
    kv_pg = kv_pg.reshape(B, L, 4, N_KV_HEADS, HEAD_DIM)
    k_sel = _rope(_rms(kv_pg[:, :, SLOT_KSEL], qk_g[2]), pos)
    rows_pg = jnp.stack([kv_pg[:, :, 0], kv_pg[:, :, 1], k_sel, kv_pg[:, :, SLOT_VSEL]], axis=2)
    kv_win = kv_win.reshape(B, L, 2, N_KV_HEADS, HEAD_DIM)
    k_win = _rope(_rms(kv_win[:, :, 0], qk_g[3]), pos)
    rows_win = jnp.stack([k_win, kv_win[:, :, 1]], axis=2)
    gates = jax.nn.sigmoid(g.reshape(B, L, N_KV_HEADS, GQA, N_BRANCH))
    return q_raw, q_rot, rows_pg, rows_win, gates


def _compress(raw, pe, w1, w2, gk):
    B, L = raw.shape[:2]
    nc = L // CMP_STRIDE
    ch = raw[:, :nc * CMP_STRIDE].reshape(B, nc, CMP_STRIDE, 2, N_KV_HEADS, HEAD_DIM)
    lead = jnp.einsum('bnlskd,sldh->bnskh', ch + pe[:CMP_STRIDE, :, None, :], w1[:, :CMP_STRIDE])
    tail = jnp.einsum('bnlskd,sldh->bnskh', ch + pe[CMP_STRIDE:, :, None, :], w1[:, CMP_STRIDE:])
    hid = jax.nn.gelu(lead[:, :-1] + tail[:, 1:])
    out = jnp.einsum('bnskh,she->bnske', hid, w2)
    return _rms(out[:, :, 0], gk), out[:, :, 1]


def _overlap(n_cmp, n_sel):
    cstart = jnp.arange(n_cmp)[:, None] * CMP_STRIDE
    sstart = jnp.arange(n_sel)[None, :] * SEL_BLOCK
    return ((cstart < sstart + SEL_BLOCK) & (cstart + CMP_BLOCK > sstart)).astype(jnp.float32)


def _nsa_attend(q_raw, q_rot, qpos, kc, vc, fetch, n_sel, kw, vw, kwpos, gates):
    scale = HEAD_DIM ** -0.5
    n_cmp = kc.shape[1]
    cmp_end = jnp.arange(n_cmp) * CMP_STRIDE + (CMP_BLOCK - 1)
    m_cmp = (cmp_end[None, :] <= qpos[:, None])[None, :, None, None, :]
    s = jnp.einsum('bqkgd,bnkd->bqkgn', q_raw, kc).astype(jnp.float32) * scale
    p_cmp = _masked_probs(s, m_cmp)
    o_cmp = jnp.einsum('bqkgn,bnkd->bqkgd', p_cmp.astype(vc.dtype), vc)
    imp = jnp.einsum('bqkn,ns->bqks', jnp.sum(p_cmp, axis=3), _overlap(n_cmp, n_sel))
    blk = jnp.arange(n_sel)[None, :]
    qblk = (qpos // SEL_BLOCK)[:, None]
    valid = blk * SEL_BLOCK <= qpos[:, None]
    forced = valid & ((blk == 0) | (blk == qblk) | (blk == qblk - 1))
    imp = jnp.where(forced[None, :, None, :], FORCE_SCORE, jnp.where(valid[None, :, None, :], imp, NEG_INF))
    _, idx = lax.top_k(imp, TOP_N)
    ks, vs, kpos = fetch(idx)
    s = jnp.einsum('bqkgd,bqknd->bqkgn', q_rot, ks).astype(jnp.float32) * scale
    m_sel = (kpos <= qpos[None, :, None, None])[:, :, :, None, :]
    p = _masked_probs(s, m_sel)
    o_sel = jnp.einsum('bqkgn,bqknd->bqkgd', p.astype(vs.dtype), vs)
    s = jnp.einsum('bqkgd,blkd->bqkgl', q_rot, kw).astype(jnp.float32) * scale
    kp = kwpos[None, :]
    m_win = ((kp <= qpos[:, None]) & (kp > qpos[:, None] - WINDOW) & (kp >= 0))[None, :, None, None, :]
    p = _masked_probs(s, m_win)
    o_win = jnp.einsum('bqkgl,blkd->bqkgd', p.astype(vw.dtype), vw)
    return gates[..., 0:1] * o_cmp + gates[..., 1:2] * o_sel + gates[..., 2:3] * o_win


def _nsa_prompt(h, w_in, w_o, qk_g, pe, w1, w2, win_buf):
    B, S, _ = h.shape
    pos = jnp.arange(S)
    q_raw, q_rot, rows_pg, rows_win, gates = _nsa_project(h, pos, w_in, qk_g)
    kc, vc = _compress(rows_pg[:, :, 0:2], pe, w1, w2, qk_g[1])
    k_sel = rows_pg[:, :, SLOT_KSEL]
    v_sel = rows_pg[:, :, SLOT_VSEL]
    n_sel = max(-(-S // SEL_BLOCK), TOP_N)
    bi = jnp.arange(B)[:, None, None, None]
    hi = jnp.arange(N_KV_HEADS)[None, None, :, None]

    def fetch(idx):
        kpos = (idx[..., None] * SEL_BLOCK + jnp.arange(SEL_BLOCK)).reshape(idx.shape[:3] + (TOP_N * SEL_BLOCK,))
        pc = jnp.minimum(kpos, S - 1)
        return k_sel[bi, pc, hi], v_sel[bi, pc, hi], kpos

    kw_pad = jnp.pad(rows_win, ((0, 0), (WINDOW, 0), (0, 0), (0, 0), (0, 0)))

    def block(i):
        s0 = i * NSA_Q_BLOCK
        sl = lambda a: lax.dynamic_slice_in_dim(a, s0, NSA_Q_BLOCK, axis=1)
        kw = lax.dynamic_slice_in_dim(kw_pad, s0, WINDOW + NSA_Q_BLOCK, axis=1)
        kwpos = s0 - WINDOW + jnp.arange(WINDOW + NSA_Q_BLOCK)
        qpos = s0 + jnp.arange(NSA_Q_BLOCK)
        return _nsa_attend(sl(q_raw), sl(q_rot), qpos, kc, vc, fetch, n_sel,
                           kw[:, :, 0], kw[:, :, 1], kwpos, sl(gates))

    o = lax.map(block, jnp.arange(S // NSA_Q_BLOCK))
    o = jnp.moveaxis(o, 0, 1).reshape(B, S, Q_COLS)
    y = o @ w_o
    new_pg = rows_pg.reshape(B, S // PAGE_SIZE, PAGE_SIZE, 4, N_KV_HEADS, HEAD_DIM)
    new_win = rows_win[:, max(S - win_buf, 0):]
    if S < win_buf:
        new_win = jnp.pad(new_win, ((0, 0), (win_buf - S, 0), (0, 0), (0, 0), (0, 0)))
    return y, new_pg, new_win


def _nsa_sample(h, cache_kv, j, win_state, page_table, w_in, w_o, qk_g, pe, w1, w2):
    B, Q, _ = h.shape
    pos = PAST_LEN + jnp.arange(Q)
    q_raw, q_rot, rows_pg, rows_win, gates = _nsa_project(h, pos, w_in, qk_g)
    past_cmp = cache_kv[j, page_table[:, :, None], jnp.arange(PAGE_SIZE)[None, None, :], 0:2]
    past_cmp = past_cmp.reshape(B, PAST_LEN, 2, N_KV_HEADS, HEAD_DIM)
    kc, vc = _compress(jnp.concatenate([past_cmp, rows_pg[:, :, 0:2]], axis=1), pe, w1, w2, qk_g[1])
    n_sel = max(-(-(PAST_LEN + Q) // SEL_BLOCK), TOP_N)
    bi = jnp.arange(B)[:, None, None, None]
    hi = jnp.arange(N_KV_HEADS)[None, None, :, None]

    def fetch(idx):
        kpos = (idx[..., None] * SEL_BLOCK + jnp.arange(SEL_BLOCK)).reshape(idx.shape[:3] + (TOP_N * SEL_BLOCK,))
        pc = jnp.minimum(kpos, PAST_LEN - 1)
        phys = page_table[bi, pc // PAGE_SIZE]
        off = pc % PAGE_SIZE
        nc = jnp.clip(kpos - PAST_LEN, 0, Q - 1)
        is_past = (kpos < PAST_LEN)[..., None]
        k = jnp.where(is_past, cache_kv[j, phys, off, SLOT_KSEL, hi], rows_pg[bi, nc, SLOT_KSEL, hi])
        v = jnp.where(is_past, cache_kv[j, phys, off, SLOT_VSEL, hi], rows_pg[bi, nc, SLOT_VSEL, hi])
        return k, v, kpos

    wb = win_state.shape[1]
    kw_all = jnp.concatenate([win_state, rows_win], axis=1)
    kwpos = PAST_LEN - wb + jnp.arange(wb + Q)
    o = _nsa_attend(q_raw, q_rot, pos, kc, vc, fetch, n_sel, kw_all[:, :, 0], kw_all[:, :, 1], kwpos, gates)
    y = o.reshape(B, Q, Q_COLS) @ w_o
    return y, rows_pg, kw_all[:, Q:]


def _moe(h, router_w, w1, w3, w2, layer):
    B, L, D = h.shape
    T = B * L
    x = h.reshape(T, D)
    probs = jax.nn.softmax((x @ router_w).astype(jnp.float32), axis=-1)
    grouped = probs.reshape(T, N_EXPERT_GROUPS, EXPERTS_PER_GROUP)
    gscore = jnp.sum(lax.top_k(grouped, GROUP_SCORE_TOPK)[0], axis=-1)
    grp = jnp.argmax(gscore, axis=-1)
    in_grp = grouped[jnp.arange(T), grp]
    top_p, top_i = lax.top_k(in_grp, TOP_K)
    wts = top_p / jnp.sum(top_p, axis=-1, keepdims=True)
    eid = (grp[:, None] * EXPERTS_PER_GROUP + top_i).reshape(-1)
    tok = jnp.repeat(jnp.arange(T), TOP_K)
    wflat = wts.reshape(-1)
    A = T * TOP_K
    blk = MOE_BLOCK if A >= MOE_BLOCK * N_EXPERTS else MOE_BLOCK_SMALL
    order = jnp.argsort(eid)
    e_s, t_s, w_s = eid[order], tok[order], wflat[order]
    counts = jnp.zeros((N_EXPERTS,), jnp.int32).at[eid].add(1)
    padded = (counts + blk - 1) // blk * blk
    start = jnp.cumsum(counts) - counts
    pstart = jnp.cumsum(padded) - padded
    pend = pstart + padded
    dest = pstart[e_s] + jnp.arange(A) - start[e_s]
    n_blk = (A + N_EXPERTS * (blk - 1) + blk - 1) // blk
    xbuf = jnp.zeros((n_blk * blk, D), x.dtype).at[dest].set(x[t_s])
    blk_exp = jnp.minimum(jnp.sum(pend[None, :] <= (jnp.arange(n_blk) * blk)[:, None], axis=1), N_EXPERTS - 1)

    def expert_block(args):
        xb, e = args
        return (jax.nn.silu(xb @ w1[layer, e]) * (xb @ w3[layer, e])) @ w2[layer, e]

    ybuf = lax.map(expert_block, (xbuf.reshape(n_blk, blk, D), blk_exp)).reshape(n_blk * blk, D)
    y = ybuf[dest] * w_s[:, None].astype(x.dtype)
    return jnp.zeros((T, D), x.dtype).at[t_s].add(y).reshape(B, L, D)


def setup_inputs(seed: int = 0) -> dict:
    key = jax.random.key(seed)
    ks = jax.random.split(key, 24)
    f32 = jnp.float32

    def nrm(k, shape, s=1.0):
        return jax.random.normal(k, shape, f32) * s

    n_pages = PAST_LEN // PAGE_SIZE
    n_used = DEC_BATCH * n_pages
    n_phys = n_used + max(n_used // 4, 1)
    win_buf = min(WINDOW, PAST_LEN)
    page_table = jax.random.permutation(ks[5], n_phys)[:n_used].reshape(DEC_BATCH, n_pages).astype(jnp.int32)
    return {
        'x_prompt': nrm(ks[0], (BATCH, SEQ, D_MODEL)),
        'x_sample': nrm(ks[1], (DEC_BATCH, DEC_SEQ, D_MODEL)),
        'cache_kv': nrm(ks[2], (N_NSA_LAYERS, n_phys, PAGE_SIZE, 4, N_KV_HEADS, HEAD_DIM)),
        'state_win': nrm(ks[3], (N_NSA_LAYERS, DEC_BATCH, win_buf, 2, N_KV_HEADS, HEAD_DIM)),
        'state_pool': nrm(ks[4], (N_POOL_LAYERS, DEC_BATCH, POOL_BUF, D_MODEL)),
        'page_table': page_table,
        'c_prompt': nrm(ks[6], (BATCH, D_MODEL)),
        'c_sample': nrm(ks[7], (DEC_BATCH, D_MODEL)),
        'ada_w': nrm(ks[8], (DEPTH, D_MODEL, 6 * D_MODEL), ADA_INIT * D_MODEL ** -0.5),
        'ada_b': nrm(ks[9], (DEPTH, 6 * D_MODEL), 0.01),
        'norm_g': 1.0 + nrm(ks[10], (DEPTH, 2, D_MODEL), 0.02),
        'pool_w': nrm(ks[11], (N_POOL_LAYERS, POOL_GROUPS, POOL_CG, POOL_CG), POOL_CG ** -0.5),
        'pool_scale': 1.0 + nrm(ks[12], (N_POOL_LAYERS, D_MODEL), 0.1),
        'nsa_w_in': nrm(ks[13], (N_NSA_LAYERS, D_MODEL, IN_COLS), D_MODEL ** -0.5),
        'nsa_w_o': nrm(ks[14], (N_NSA_LAYERS, Q_COLS, D_MODEL), Q_COLS ** -0.5),
        'nsa_qk_g': 1.0 + nrm(ks[15], (N_NSA_LAYERS, 4, HEAD_DIM), 0.02),
        'nsa_cmp_pe': nrm(ks[16], (N_NSA_LAYERS, CMP_BLOCK, 2, HEAD_DIM), 0.1),
        'nsa_cmp_w1': nrm(ks[17], (N_NSA_LAYERS, 2, CMP_BLOCK, HEAD_DIM, CMP_HIDDEN), (CMP_BLOCK * HEAD_DIM) ** -0.5),
        'nsa_cmp_w2': nrm(ks[18], (N_NSA_LAYERS, 2, CMP_HIDDEN, HEAD_DIM), CMP_HIDDEN ** -0.5),
        'router_w': nrm(ks[19], (D_MODEL, N_EXPERTS), D_MODEL ** -0.5),
        'moe_w1': nrm(ks[20], (DEPTH, N_EXPERTS, D_MODEL, EXPERT_FF), D_MODEL ** -0.5),
        'moe_w3': nrm(ks[21], (DEPTH, N_EXPERTS, D_MODEL, EXPERT_FF), D_MODEL ** -0.5),
        'moe_w2': nrm(ks[22], (DEPTH, N_EXPERTS, EXPERT_FF, D_MODEL), EXPERT_FF ** -0.5),
    }


def reference(x_prompt, x_sample, cache_kv, state_win, state_pool, page_table, c_prompt, c_sample,
              ada_w, ada_b, norm_g, pool_w, pool_scale, nsa_w_in, nsa_w_o, nsa_qk_g, nsa_cmp_pe,
              nsa_cmp_w1, nsa_cmp_w2, router_w, moe_w1, moe_w3, moe_w2):
    xp, xs = x_prompt, x_sample
    win_buf = state_win.shape[2]
    kv_p, kv_s, win_p, win_s, pool_p, pool_s = [], [], [], [], [], []
    for i in range(DEPTH):
        j = i // N_MIXERS
        sp1, cp1, gp1, sp2, cp2, gp2 = _ada(c_prompt, ada_w[i], ada_b[i])
        ss1, cs1, gs1, ss2, cs2, gs2 = _ada(c_sample, ada_w[i], ada_b[i])
        hp = _modulate(xp, norm_g[i, 0], sp1, cp1)
        hs = _modulate(xs, norm_g[i, 0], ss1, cs1)
        if i % N_MIXERS == 0:
            mp = _pool_mix(hp, 0, hp.shape[1], pool_w[j], pool_scale[j])
            full_s = jnp.concatenate([state_pool[j], hs], axis=1)
            ms = _pool_mix(full_s, PAST_LEN - POOL_BUF, hs.shape[1], pool_w[j], pool_scale[j])
            pool_p.append(hp[:, -POOL_BUF:])
            pool_s.append(full_s[:, -POOL_BUF:])
        else:
            mp, kvp, wp = _nsa_prompt(hp, nsa_w_in[j], nsa_w_o[j], nsa_qk_g[j], nsa_cmp_pe[j],
                                      nsa_cmp_w1[j], nsa_cmp_w2[j], win_buf)
            ms, kvs, ws = _nsa_sample(hs, cache_kv, j, state_win[j], page_table, nsa_w_in[j], nsa_w_o[j],
                                      nsa_qk_g[j], nsa_cmp_pe[j], nsa_cmp_w1[j], nsa_cmp_w2[j])
            kv_p.append(kvp)
            kv_s.append(kvs)
            win_p.append(wp)
            win_s.append(ws)
        xp = xp + gp1 * mp
        xs = xs + gs1 * ms
        hp = _modulate(xp, norm_g[i, 1], sp2, cp2)
        hs = _modulate(xs, norm_g[i, 1], ss2, cs2)
        xp = xp + gp2 * _moe(hp, router_w, moe_w1, moe_w3, moe_w2, i)
        xs = xs + gs2 * _moe(hs, router_w, moe_w1, moe_w3, moe_w2, i)
    return (xp, xs, jnp.stack(kv_p), jnp.stack(kv_s), jnp.stack(win_p), jnp.stack(win_s),
            jnp.stack(pool_p), jnp.stack(pool_s))
```

```python
import functools

import jax
import jax.numpy as jnp
from jax import lax
from jax.experimental import pallas as pl
from jax.experimental.pallas import tpu as pltpu

F32 = jnp.float32
BF16 = jnp.bfloat16
I32 = jnp.int32

D_MODEL = 4096
PAST_LEN = 16384
PAGE_SIZE = 128
POOL_WINDOWS = (2, 4, 8, 16)
POOL_GROUPS = 4
POOL_CG = D_MODEL // POOL_GROUPS
POOL_BUF = max(POOL_WINDOWS) - 1
POOL_HALO = 16
HEAD_DIM = 128
N_HEADS = D_MODEL // HEAD_DIM
N_KV_HEADS = 4
GQA = N_HEADS // N_KV_HEADS
CMP_STRIDE = 16
CMP_BLOCK = 2 * CMP_STRIDE
SEL_BLOCK = 64
TOP_N = 16
WINDOW = 512
N_BRANCH = 3
ROPE_THETA = 10000.0
Q_COLS = N_HEADS * HEAD_DIM
SLOT_COLS = N_KV_HEADS * HEAD_DIM
PG_COLS = 4 * SLOT_COLS
WIN_COLS = 2 * SLOT_COLS
GATE_COLS = N_BRANCH * N_HEADS
FORCE_SCORE = 1e4
NEG_INF = -1e30
TINY = 1e-30
N_EXPERTS = 64
N_EXPERT_GROUPS = 8
EXPERTS_PER_GROUP = N_EXPERTS // N_EXPERT_GROUPS
TOP_K = 2
EXPERT_FF = 1024
NORM_EPS = 1e-6
ATTN_SCALE = HEAD_DIM ** -0.5

SAMPLE_ROWS = 16
VMEM_LIMIT = 56 * 2 ** 20

_NT = (((1,), (1,)), ((), ()))


def _params(sem, vmem=VMEM_LIMIT):
    return pltpu.CompilerParams(dimension_semantics=sem, vmem_limit_bytes=vmem)


def _modulate(x, g, shift, scale):
    ms = jnp.mean(x * x, axis=-1, keepdims=True)
    return (x * lax.rsqrt(ms + NORM_EPS) * g) * (1.0 + scale) + shift


def _head_norm(y, g):
    ms = jnp.mean(y * y, axis=-1, keepdims=True)
    return y * lax.rsqrt(ms + NORM_EPS) * g


def _rope(y, cosf, sinf):
    return y * cosf + pltpu.roll(y, HEAD_DIM // 2, axis=1) * sinf


def _ada_kernel(c_ref, w_ref, b_ref, o_ref, *, kc):
    c = c_ref[...]
    a = (c * jax.nn.sigmoid(c)).astype(BF16)
    acc = jnp.zeros(o_ref.shape[1:], F32)
    for k0 in range(0, D_MODEL, kc):
        acc = acc + jnp.dot(a[:, k0:k0 + kc], w_ref[0, k0:k0 + kc, :].astype(BF16),
                            preferred_element_type=F32)
    o_ref[0] = acc + b_ref[0]


def _ada(c_all, ada_w, ada_b):
    depth, _, n6 = ada_w.shape
    rows = c_all.shape[0]
    tn = 512
    return pl.pallas_call(
        functools.partial(_ada_kernel, kc=512),
        out_shape=jax.ShapeDtypeStruct((depth, rows, n6), F32),
        grid=(depth, n6 // tn),
        in_specs=[
            pl.BlockSpec((rows, D_MODEL), lambda l, j: (0, 0)),
            pl.BlockSpec((1, D_MODEL, tn), lambda l, j: (l, 0, j)),
            pl.BlockSpec((1, 1, tn), lambda l, j: (l, 0, j)),
        ],
        out_specs=pl.BlockSpec((1, rows, tn), lambda l, j: (l, 0, j)),
        compiler_params=_params(("arbitrary", "arbitrary")),
    )(c_all, ada_w, ada_b.reshape(depth, 1, n6))


def _pool_kernel(x_ref, sh_ref, sc_ref, gt_ref, g_ref, w_ref, ps_ref, o_ref, tail_ref, hbuf, pbuf,
                 *, ts, rb):
    i = pl.program_id(1)

    @pl.when(i == 0)
    def _():
        hbuf[0:POOL_HALO, :] = jnp.zeros((POOL_HALO, D_MODEL), F32)

    def norm_rows(r, c):
        r0 = pl.multiple_of(r * rb, rb)
        h = _modulate(x_ref[0, pl.ds(r0, rb), :], g_ref[...], sh_ref[0], sc_ref[0])
        hbuf[pl.ds(POOL_HALO + r0, rb), :] = h
        return c

    lax.fori_loop(0, ts // rb, norm_rows, 0)

    for r0 in range(0, ts, rb):
        pos = i * ts + r0 + lax.broadcasted_iota(I32, (rb, 1), 0)
        for gi, win in enumerate(POOL_WINDOWS):
            c0, c1 = gi * POOL_CG, (gi + 1) * POOL_CG
            h0 = hbuf[POOL_HALO + r0:POOL_HALO + r0 + rb, c0:c1]
            acc = h0
            for k in range(1, win):
                acc = acc + hbuf[POOL_HALO + r0 - k:POOL_HALO + r0 - k + rb, c0:c1]
            cnt = jnp.minimum(pos + 1, win).astype(F32)
            pbuf[r0:r0 + rb, c0:c1] = (acc / cnt - h0).astype(BF16)

    for gi in range(POOL_GROUPS):
        c0, c1 = gi * POOL_CG, (gi + 1) * POOL_CG
        m = jnp.dot(pbuf[:, c0:c1], w_ref[gi], preferred_element_type=F32)
        o_ref[0, :, c0:c1] = x_ref[0, :, c0:c1] + gt_ref[0, :, c0:c1] * (m * ps_ref[:, c0:c1])

    tail = hbuf[ts:ts + POOL_HALO, :]
    tail_ref[0] = tail
    hbuf[0:POOL_HALO, :] = tail


def _pool_prompt(x, shift, scale, gate, g, pool_w_bf, pool_scale):
    B, S, D = x.shape
    ts = 256
    mod = pl.BlockSpec((1, 1, D), lambda b, i: (b, 0, 0))
    row = pl.BlockSpec((1, D), lambda b, i: (0, 0))
    return pl.pallas_call(
        functools.partial(_pool_kernel, ts=ts, rb=32),
        out_shape=(jax.ShapeDtypeStruct((B, S, D), F32), jax.ShapeDtypeStruct((B, POOL_HALO, D), F32)),
        grid=(B, S // ts),
        in_specs=[
            pl.BlockSpec((1, ts, D), lambda b, i: (b, i, 0)), mod, mod, mod, row,
            pl.BlockSpec((POOL_GROUPS, POOL_CG, POOL_CG), lambda b, i: (0, 0, 0)), row,
        ],
        out_specs=(pl.BlockSpec((1, ts, D), lambda b, i: (b, i, 0)),
                   pl.BlockSpec((1, POOL_HALO, D), lambda b, i: (b, 0, 0))),
        scratch_shapes=[pltpu.VMEM((POOL_HALO + ts, D), F32), pltpu.VMEM((ts, D), BF16)],
        compiler_params=_params(("arbitrary", "arbitrary")),
    )(x, shift, scale, gate, g, pool_w_bf, pool_scale)


def _pool_sample_kernel(x_ref, st_ref, sh_ref, sc_ref, gt_ref, g_ref, w_ref, ps_ref, o_ref, h_ref):
    x = x_ref[...]
    h = _modulate(x, g_ref[...], sh_ref[...], sc_ref[...])
    h_ref[...] = h
    for gi, win in enumerate(POOL_WINDOWS):
        c0, c1 = gi * POOL_CG, (gi + 1) * POOL_CG
        h0 = h[:, c0:c1]
        acc = h0
        for k in range(1, win):
            acc = acc + st_ref[POOL_BUF - k, :, c0:c1]
        cnt = float(min(PAST_LEN + 1, win))
        pooled = (acc / cnt - h0).astype(BF16)
        m = jnp.dot(pooled, w_ref[gi], preferred_element_type=F32)
        o_ref[:, c0:c1] = x[:, c0:c1] + gt_ref[:, c0:c1] * (m * ps_ref[:, c0:c1])


def _pool_sample(x, state_t, shift, scale, gate, g, pool_w_bf, pool_scale):
    R, D = x.shape
    return pl.pallas_call(
        _pool_sample_kernel,
        out_shape=(jax.ShapeDtypeStruct((R, D), F32), jax.ShapeDtypeStruct((R, D), F32)),
        compiler_params=_params(None),
    )(x, state_t, shift, scale, gate, g, pool_w_bf, pool_scale)


def _router_kernel(x_ref, sh_ref, sc_ref, g_ref, rh_ref, rl_ref, h_ref, r_ref):
    h = _modulate(x_ref[0], g_ref[...], sh_ref[0], sc_ref[0])
    h_ref[0] = h
    hh = h.astype(BF16)
    hl = (h - hh.astype(F32)).astype(BF16)
    rh, rl = rh_ref[...], rl_ref[...]
    logits = (lax.dot_general(rh, hh, _NT, preferred_element_type=F32)
              + lax.dot_general(rh, hl, _NT, preferred_element_type=F32)
              + lax.dot_general(rl, hh, _NT, preferred_element_type=F32))
    tm = logits.shape[1]
    mx = jnp.max(logits, axis=0, keepdims=True)
    e = jnp.exp(logits - mx)
    probs = e / jnp.sum(e, axis=0, keepdims=True)
    sub = lax.broadcasted_iota(I32, (EXPERTS_PER_GROUP, tm), 0)

    def top2(p):
        v1 = jnp.max(p, axis=0, keepdims=True)
        i1 = jnp.min(jnp.where(p == v1, sub, EXPERTS_PER_GROUP), axis=0, keepdims=True)
        p2 = jnp.where(sub == i1, -1.0, p)
        v2 = jnp.max(p2, axis=0, keepdims=True)
        i2 = jnp.min(jnp.where(p2 == v2, sub, EXPERTS_PER_GROUP), axis=0, keepdims=True)
        return v1, i1, v2, i2

    groups = [probs[gi * EXPERTS_PER_GROUP:(gi + 1) * EXPERTS_PER_GROUP, :] for gi in range(N_EXPERT_GROUPS)]
    best = None
    for gi, p in enumerate(groups):
        v1, _, v2, _ = top2(p)
        score = v1 + v2
        if best is None:
            best, grp = score, jnp.zeros((1, tm), I32)
        else:
            better = score > best
            grp = jnp.where(better, gi, grp)
            best = jnp.where(better, score, best)
    in_grp = jnp.zeros((EXPERTS_PER_GROUP, tm), F32)
    for gi, p in enumerate(groups):
        in_grp = in_grp + jnp.where(grp == gi, p, 0.0)
    v1, i1, v2, i2 = top2(in_grp)
    tot = v1 + v2
    r_ref[0:1, :] = (grp * EXPERTS_PER_GROUP + i1).astype(F32)
    r_ref[1:2, :] = (grp * EXPERTS_PER_GROUP + i2).astype(F32)
    r_ref[2:3, :] = v1 / tot
    r_ref[3:4, :] = v2 / tot
    r_ref[4:8, :] = jnp.zeros((4, tm), F32)


def _router(x, shift, scale, g, rw_hi, rw_lo, tm):
    NB, L, D = x.shape
    GR = shift.shape[1]
    nl = L // tm
    mod = pl.BlockSpec((1, GR, D), lambda b, i: (b, 0, 0))
    return pl.pallas_call(
        _router_kernel,
        out_shape=(jax.ShapeDtypeStruct((NB, L, D), F32), jax.ShapeDtypeStruct((8, NB * L), F32)),
        grid=(NB, nl),
        in_specs=[
            pl.BlockSpec((1, tm, D), lambda b, i: (b, i, 0)), mod, mod,
            pl.BlockSpec((1, D), lambda b, i: (0, 0)),
            pl.BlockSpec((N_EXPERTS, D), lambda b, i: (0, 0)),
            pl.BlockSpec((N_EXPERTS, D), lambda b, i: (0, 0)),
        ],
        out_specs=(pl.BlockSpec((1, tm, D), lambda b, i: (b, i, 0)),
                   pl.BlockSpec((8, tm), lambda b, i: (0, b * nl + i))),
        compiler_params=_params(("arbitrary", "arbitrary")),
    )(x, shift, scale, g, rw_hi, rw_lo)


def _expert_kernel(texp, tstart, tnch, nused, tok, h_hbm, w1_ref, w3_ref, w2_ref, o_ref,
                   xbuf, w1b, w3b, w2b, sem, *, tm, ch, wr):
    s = pl.program_id(0)
    f = pl.program_id(1)
    nch = tnch[s]
    active = s < nused[0]

    def row_copy(i):
        t = tok[tstart[s] + i]
        return pltpu.make_async_copy(h_hbm.at[pl.ds(t, 1)], xbuf.at[pl.ds(i, 1)], sem)

    @pl.when(jnp.logical_and(active, f == 0))
    def _gather():
        def issue(i, c):
            row_copy(i).start()
            return c

        def drain(i, c):
            row_copy(i).wait()
            return c

        lax.fori_loop(0, nch * ch, issue, 0)
        lax.fori_loop(0, nch * ch, drain, 0)

    @pl.when(active)
    def _compute():
        def cast_rows(r, c):
            r0 = pl.multiple_of(r * wr, wr)
            w1b[pl.ds(r0, wr), :] = w1_ref[pl.ds(r0, wr), :].astype(BF16)
            w3b[pl.ds(r0, wr), :] = w3_ref[pl.ds(r0, wr), :].astype(BF16)
            return c

        lax.fori_loop(0, D_MODEL // wr, cast_rows, 0)
        w2b[...] = w2_ref[...].astype(BF16)

        def chunk(c, carry):
            r0 = pl.multiple_of(c * ch, ch)
            xb = xbuf[pl.ds(r0, ch), :].astype(BF16)
            h1 = jnp.dot(xb, w1b[...], preferred_element_type=F32)
            h3 = jnp.dot(xb, w3b[...], preferred_element_type=F32)
            hh = (h1 * jax.nn.sigmoid(h1) * h3).astype(BF16)
            y = jnp.dot(hh, w2b[...], preferred_element_type=F32)

            @pl.when(f == 0)
            def _():
                o_ref[pl.ds(r0, ch), :] = y

            @pl.when(f != 0)
            def _():
                o_ref[pl.ds(r0, ch), :] += y

            return carry

        lax.fori_loop(0, nch, chunk, 0)

        @pl.when(f == 0)
        def _():
            def zero(c, carry):
                r0 = pl.multiple_of(c * ch, ch)
                o_ref[pl.ds(r0, ch), :] = jnp.zeros((ch, D_MODEL), F32)
                return carry

            lax.fori_loop(nch, tm // ch, zero, 0)


def _experts(plan, h_rows, w1, w3, w2, layer, *, tm, ch, tf):
    nt = plan["texp"].shape[0]
    nf = EXPERT_FF // tf
    D = D_MODEL

    def w13_map(s, f, texp, tstart, tnch, nused, tok):
        return (layer, texp[s], 0, jnp.where(s < nused[0], f, nf - 1))

    def w2_map(s, f, texp, tstart, tnch, nused, tok):
        return (layer, texp[s], jnp.where(s < nused[0], f, nf - 1), 0)

    def o_map(s, f, texp, tstart, tnch, nused, tok):
        return (jnp.minimum(s, nused[0] - 1), 0)

    grid_spec = pltpu.PrefetchScalarGridSpec(
        num_scalar_prefetch=5,
        grid=(nt, nf),
        in_specs=[
            pl.BlockSpec(memory_space=pl.ANY),
            pl.BlockSpec((None, None, D, tf), w13_map),
            pl.BlockSpec((None, None, D, tf), w13_map),
            pl.BlockSpec((None, None, tf, D), w2_map),
        ],
        out_specs=pl.BlockSpec((tm, D), o_map),
        scratch_shapes=[
            pltpu.VMEM((tm, D), F32), pltpu.VMEM((D, tf), BF16), pltpu.VMEM((D, tf), BF16),
            pltpu.VMEM((tf, D), BF16), pltpu.SemaphoreType.DMA(()),
        ],
    )
    return pl.pallas_call(
        functools.partial(_expert_kernel, tm=tm, ch=ch, wr=256),
        out_shape=jax.ShapeDtypeStruct((nt * tm, D), F32),
        grid_spec=grid_spec,
        compiler_params=_params(("arbitrary", "arbitrary")),
    )(plan["texp"], plan["tstart"], plan["tnch"], plan["nused"], plan["tok"], h_rows, w1, w3, w2)


def _combine_kernel(dest, x_ref, gt_ref, w_ref, y_hbm, o_ref, ybv, sem, *, tc, nl):
    t0 = (pl.program_id(0) * nl + pl.program_id(1)) * tc

    def row_copy(r, k):
        d = dest[(t0 + r) * TOP_K + k]
        return pltpu.make_async_copy(y_hbm.at[pl.ds(d, 1)], ybv.at[k, pl.ds(r, 1)], sem)

    def issue(r, c):
        for k in range(TOP_K):
            row_copy(r, k).start()
        return c

    def drain(r, c):
        for k in range(TOP_K):
            row_copy(r, k).wait()
        return c

    lax.fori_loop(0, tc, issue, 0)
    lax.fori_loop(0, tc, drain, 0)
    w = w_ref[...]
    y = ybv[0] * w[:, 0:1] + ybv[1] * w[:, 1:2]
    o_ref[0] = x_ref[0] + gt_ref[0] * y


def _combine(x, gate, wts, dest, ybuf, tc):
    NB, L, D = x.shape
    GR = gate.shape[1]
    nl = L // tc
    grid_spec = pltpu.PrefetchScalarGridSpec(
        num_scalar_prefetch=1,
        grid=(NB, nl),
        in_specs=[
            pl.BlockSpec((1, tc, D), lambda b, i, d: (b, i, 0)),
            pl.BlockSpec((1, GR, D), lambda b, i, d: (b, 0, 0)),
            pl.BlockSpec((tc, TOP_K), lambda b, i, d: (b * nl + i, 0)),
            pl.BlockSpec(memory_space=pl.ANY),
        ],
        out_specs=pl.BlockSpec((1, tc, D), lambda b, i, d: (b, i, 0)),
        scratch_shapes=[pltpu.VMEM((TOP_K, tc, D), F32), pltpu.SemaphoreType.DMA(())],
    )
    return pl.pallas_call(
        functools.partial(_combine_kernel, tc=tc, nl=nl),
        out_shape=jax.ShapeDtypeStruct((NB, L, D), F32),
        grid_spec=grid_spec,
        compiler_params=_params(("arbitrary", "arbitrary")),
    )(dest, x, gate, wts, ybuf)


def _moe_plan(eid, n_rows, tm, ch, nt):
    T = eid.shape[0]
    A = T * TOP_K
    e_flat = eid.reshape(A)
    real = (jnp.arange(A) // TOP_K) < n_rows
    onehot = jnp.logical_and(e_flat[:, None] == jnp.arange(N_EXPERTS)[None, :], real[:, None])
    csum = jnp.cumsum(onehot.astype(I32), axis=0)
    rank = jnp.sum(jnp.where(onehot, csum - 1, 0), axis=1)
    counts = csum[-1]
    ntile_e = (counts + tm - 1) // tm
    tile_end = jnp.cumsum(ntile_e)
    tile_first = tile_end - ntile_e
    start_e = jnp.cumsum(counts) - counts
    nused = tile_end[-1]
    tidx = jnp.arange(nt)
    texp = jnp.minimum(jnp.sum((tile_end[None, :] <= tidx[:, None]).astype(I32), axis=1), N_EXPERTS - 1)
    within = tidx - tile_first[texp]
    tstart = start_e[texp] + within * tm
    nvalid = jnp.clip(counts[texp] - within * tm, 0, tm)
    used = tidx < nused
    tnch = jnp.where(used, (nvalid + ch - 1) // ch, 0)
    texp = jnp.where(used, texp, texp[jnp.maximum(nused - 1, 0)])
    tstart = jnp.where(used, tstart, 0)
    spos = jnp.where(real, start_e[e_flat] + rank, A + tm)
    tok = jnp.zeros((A + tm,), I32).at[spos].set(jnp.arange(A, dtype=I32) // TOP_K, mode="drop")
    dest = (tile_first[e_flat] + rank // tm) * tm + rank % tm
    dest = jnp.where(real, dest, 0)
    return dict(texp=texp.astype(I32), tstart=tstart.astype(I32), tnch=tnch.astype(I32),
                nused=nused.reshape(1).astype(I32), tok=tok, dest=dest.astype(I32))


def _moe_block(x, shift, scale, gate, g, rw_hi, rw_lo, w1, w3, w2, layer, *, n_rows, tm_r, tm_e, ch, tc):
    NB, L, D = x.shape
    T = NB * L
    h, route = _router(x, shift, scale, g, rw_hi, rw_lo, tm_r)
    eid = route[0:2].T.astype(I32)
    wts = route[2:4].T
    nt = -(-(n_rows * TOP_K) // tm_e) + min(N_EXPERTS, n_rows * TOP_K)
    plan = _moe_plan(eid, n_rows, tm_e, ch, nt)
    ybuf = _experts(plan, h.reshape(T, D), w1, w3, w2, layer, tm=tm_e, ch=ch, tf=128)
    return _combine(x, gate, wts, plan["dest"], ybuf, tc)


def _proj_q_kernel(x_ref, sh_ref, sc_ref, g_ref, w_ref, cos_ref, sin_ref, qg_ref, raw_ref, rot_ref, hb):
    @pl.when(pl.program_id(1) == 0)
    def _():
        hb[...] = _modulate(x_ref[0], g_ref[...], sh_ref[0], sc_ref[0]).astype(BF16)

    y = jnp.dot(hb[...], w_ref[...], preferred_element_type=F32)
    cosf, sinf = cos_ref[...], sin_ref[...]
    for hd in range(y.shape[1] // HEAD_DIM):
        c0, c1 = hd * HEAD_DIM, (hd + 1) * HEAD_DIM
        qn = _head_norm(y[:, c0:c1], qg_ref[...])
        raw_ref[:, c0:c1] = qn.astype(BF16)
        rot_ref[:, c0:c1] = _rope(qn, cosf, sinf).astype(BF16)


def _proj_q(x, shift, scale, g, w_q, cosf, sinf, qg, tm):
    NB, L, D = x.shape
    GR = shift.shape[1]
    nl = L // tm
    tn = SLOT_COLS
    mod = pl.BlockSpec((1, GR, D), lambda i, j: (i // nl, 0, 0))
    tab = pl.BlockSpec((tm, HEAD_DIM), lambda i, j: (i % nl, 0))
    out = pl.BlockSpec((tm, tn), lambda i, j: (i, j))
    return pl.pallas_call(
        _proj_q_kernel,
        out_shape=(jax.ShapeDtypeStruct((NB * L, Q_COLS), BF16),) * 2,
        grid=(NB * nl, Q_COLS // tn),
        in_specs=[
            pl.BlockSpec((1, tm, D), lambda i, j: (i // nl, i % nl, 0)), mod, mod,
            pl.BlockSpec((1, D), lambda i, j: (0, 0)),
            pl.BlockSpec((D, tn), lambda i, j: (0, j)), tab, tab,
            pl.BlockSpec((1, HEAD_DIM), lambda i, j: (0, 0)),
        ],
        out_specs=(out, out),
        scratch_shapes=[pltpu.VMEM((tm, D), BF16)],
        compiler_params=_params(("arbitrary", "arbitrary")),
    )(x, shift, scale, g, w_q, cosf, sinf, qg)


_KV_TILES = (PG_COLS + WIN_COLS) // SLOT_COLS


def _proj_kv_kernel(x_ref, sh_ref, sc_ref, g_ref, w_ref, cos_ref, sin_ref, qg_ref, pg_ref, win_ref, gt_ref, hb):
    j = pl.program_id(1)

    @pl.when(j == 0)
    def _():
        hb[...] = _modulate(x_ref[0], g_ref[...], sh_ref[0], sc_ref[0]).astype(BF16)

    y = jnp.dot(hb[...], w_ref[...], preferred_element_type=F32)

    def normed(gain):
        cosf, sinf = cos_ref[...], sin_ref[...]
        parts = []
        for hd in range(N_KV_HEADS):
            c0, c1 = hd * HEAD_DIM, (hd + 1) * HEAD_DIM
            parts.append(_rope(_head_norm(y[:, c0:c1], gain), cosf, sinf))
        return jnp.concatenate(parts, axis=1)

    @pl.when(jnp.logical_or(jnp.logical_or(j == 0, j == 1), j == 3))
    def _():
        pg_ref[...] = y

    @pl.when(j == 2)
    def _():
        pg_ref[...] = normed(qg_ref[0:1, :])

    @pl.when(j == 4)
    def _():
        win_ref[...] = normed(qg_ref[1:2, :])

    @pl.when(j == 5)
    def _():
        win_ref[...] = y

    @pl.when(j == _KV_TILES)
    def _():
        gt_ref[...] = jax.nn.sigmoid(y)


def _proj_kv(x, shift, scale, g, w_kv, cosf, sinf, qg2, tm):
    NB, L, D = x.shape
    GR = shift.shape[1]
    nl = L // tm
    tn = SLOT_COLS
    T = NB * L
    mod = pl.BlockSpec((1, GR, D), lambda i, j: (i // nl, 0, 0))
    tab = pl.BlockSpec((tm, HEAD_DIM), lambda i, j: (i % nl, 0))
    n_pg = PG_COLS // tn
    return pl.pallas_call(
        _proj_kv_kernel,
        out_shape=(jax.ShapeDtypeStruct((T, PG_COLS), F32), jax.ShapeDtypeStruct((T, WIN_COLS), F32),
                   jax.ShapeDtypeStruct((T, tn), F32)),
        grid=(NB * nl, _KV_TILES + 1),
        in_specs=[
            pl.BlockSpec((1, tm, D), lambda i, j: (i // nl, i % nl, 0)), mod, mod,
            pl.BlockSpec((1, D), lambda i, j: (0, 0)),
            pl.BlockSpec((D, tn), lambda i, j: (0, j)), tab, tab,
            pl.BlockSpec((2, HEAD_DIM), lambda i, j: (0, 0)),
        ],
        out_specs=(pl.BlockSpec((tm, tn), lambda i, j: (i, jnp.minimum(j, n_pg - 1))),
                   pl.BlockSpec((tm, tn), lambda i, j: (i, jnp.clip(j - n_pg, 0, WIN_COLS // tn - 1))),
                   pl.BlockSpec((tm, tn), lambda i, j: (i, 0))),
        scratch_shapes=[pltpu.VMEM((tm, D), BF16)],
        compiler_params=_params(("arbitrary", "arbitrary")),
    )(x, shift, scale, g, w_kv, cosf, sinf, qg2)


def _out_proj_kernel(o_ref, w_ref, x_ref, gt_ref, y_ref):
    y_ref[...] = x_ref[...] + gt_ref[0] * jnp.dot(o_ref[...], w_ref[...], preferred_element_type=F32)


def _out_proj(o, w_o, x, gate, tm):
    NB, L, D = x.shape
    GR = gate.shape[1]
    nl = L // tm
    tn = 512
    return pl.pallas_call(
        _out_proj_kernel,
        out_shape=jax.ShapeDtypeStruct((NB * L, D), F32),
        grid=(NB * nl, D // tn),
        in_specs=[
            pl.BlockSpec((tm, Q_COLS), lambda i, j: (i, 0)),
            pl.BlockSpec((Q_COLS, tn), lambda i, j: (0, j)),
            pl.BlockSpec((tm, tn), lambda i, j: (i, j)),
            pl.BlockSpec((1, GR, tn), lambda i, j: (i // nl, 0, j)),
        ],
        out_specs=pl.BlockSpec((tm, tn), lambda i, j: (i, j)),
        compiler_params=_params(("arbitrary", "arbitrary")),
    )(o, w_o, x.reshape(NB * L, D), gate).reshape(NB, L, D)


CMP_GROUP_PAGES = 16
CPP = PAGE_SIZE // CMP_STRIDE


def _cmp_stage_kernel(pages, *refs, gp):
    n_slab = 2 * N_KV_HEADS
    slabs = refs[:n_slab]
    pe_ref, w1_ref, lead_ref, tail_ref, xs = refs[n_slab:]
    p = pl.program_id(2)
    p8 = pl.multiple_of(p * CPP, CPP)
    for s in range(2):
        for l in range(CMP_STRIDE):
            for kh in range(N_KV_HEADS):
                xs[s, l, pl.ds(kh * gp * CPP + p8, CPP), :] = slabs[s * N_KV_HEADS + kh][
                    pl.ds(l, CPP, stride=CMP_STRIDE), :]

    @pl.when(p == gp - 1)
    def _():
        m = N_KV_HEADS * gp * CPP
        for s in range(2):
            lead = jnp.zeros((m, HEAD_DIM), F32)
            tail = jnp.zeros((m, HEAD_DIM), F32)
            for l in range(CMP_STRIDE):
                x = xs[s, l]
                xl = (x + pe_ref[l, s:s + 1, :]).astype(BF16)
                xt = (x + pe_ref[CMP_STRIDE + l, s:s + 1, :]).astype(BF16)
                lead = lead + jnp.dot(xl, w1_ref[s, l], preferred_element_type=F32)
                tail = tail + jnp.dot(xt, w1_ref[s, CMP_STRIDE + l], preferred_element_type=F32)
            lead_ref[0, s] = lead.reshape(N_KV_HEADS, gp * CPP, HEAD_DIM)
            tail_ref[0, s] = tail.reshape(N_KV_HEADS, gp * CPP, HEAD_DIM)


def _cmp_stage(rows2d, pages, nb, pe, w1_bf):
    npg = pages.shape[0] // nb
    gp = CMP_GROUP_PAGES
    ng = npg // gp
    nc = npg * CPP
    out = jax.ShapeDtypeStruct((nb, 2, N_KV_HEADS, nc, HEAD_DIM), F32)
    ospec = pl.BlockSpec((1, 2, N_KV_HEADS, gp * CPP, HEAD_DIM), lambda b, gq, p, pg: (b, 0, 0, gq, 0))
    n_slab = 2 * N_KV_HEADS

    def slab(cb):
        return pl.BlockSpec((PAGE_SIZE, HEAD_DIM), lambda b, gq, p, pg: (pg[b * npg + gq * gp + p], cb))

    grid_spec = pltpu.PrefetchScalarGridSpec(
        num_scalar_prefetch=1,
        grid=(nb, ng, gp),
        in_specs=[slab(cb) for cb in range(n_slab)] + [
            pl.BlockSpec((CMP_BLOCK, 2, HEAD_DIM), lambda b, gq, p, pg: (0, 0, 0)),
            pl.BlockSpec((2, CMP_BLOCK, HEAD_DIM, HEAD_DIM), lambda b, gq, p, pg: (0, 0, 0, 0)),
        ],
        out_specs=(ospec, ospec),
        scratch_shapes=[pltpu.VMEM((2, CMP_STRIDE, N_KV_HEADS * gp * CPP, HEAD_DIM), F32)],
    )
    return pl.pallas_call(
        functools.partial(_cmp_stage_kernel, gp=gp),
        out_shape=(out, out),
        grid_spec=grid_spec,
        compiler_params=_params(("arbitrary", "arbitrary", "arbitrary")),
    )(pages, *([rows2d] * n_slab), pe, w1_bf)


def _cmp_finish_kernel(lead_ref, tail_ref, w2_ref, gk_ref, o_ref, tpad, *, nc):
    s = pl.program_id(1)
    tpad[:, 0:nc, :] = tail_ref[0, 0]
    tpad[:, nc:nc + 8, :] = jnp.zeros((N_KV_HEADS, 8, HEAD_DIM), F32)
    pre = lead_ref[0, 0] + tpad[:, 1:nc + 1, :]
    hid = jax.nn.gelu(pre.reshape(N_KV_HEADS * nc, HEAD_DIM), approximate=True).astype(BF16)
    out = jnp.dot(hid, w2_ref[0], preferred_element_type=F32)
    out = jnp.where(s == 0, _head_norm(out, gk_ref[...]), out)
    n = lax.broadcasted_iota(I32, (N_KV_HEADS, nc, HEAD_DIM), 1)
    o_ref[0, 0] = jnp.where(n < nc - 1, out.reshape(N_KV_HEADS, nc, HEAD_DIM), 0.0)


def _cmp_finish(lead, tail, w2_bf, gk):
    nb, _, _, nc, _ = lead.shape
    blk = pl.BlockSpec((1, 1, N_KV_HEADS, nc, HEAD_DIM), lambda b, s: (b, s, 0, 0, 0))
    return pl.pallas_call(
        functools.partial(_cmp_finish_kernel, nc=nc),
        out_shape=jax.ShapeDtypeStruct(lead.shape, F32),
        grid=(nb, 2),
        in_specs=[blk, blk, pl.BlockSpec((1, HEAD_DIM, HEAD_DIM), lambda b, s: (s, 0, 0)),
                  pl.BlockSpec((1, HEAD_DIM), lambda b, s: (0, 0))],
        out_specs=blk,
        scratch_shapes=[pltpu.VMEM((N_KV_HEADS, nc + 8, HEAD_DIM), F32)],
        compiler_params=_params(("arbitrary", "arbitrary")),
    )(lead, tail, w2_bf, gk)


def _overlap(n_cmp_rows, n_sel_cols, n_cmp, n_sel):
    c = jnp.arange(n_cmp_rows)[:, None]
    s = jnp.arange(n_sel_cols)[None, :]
    ov = (c * CMP_STRIDE < s * SEL_BLOCK + SEL_BLOCK) & (c * CMP_STRIDE + CMP_BLOCK > s * SEL_BLOCK)
    return (ov & (c < n_cmp) & (s < n_sel)).astype(BF16)


def _rank_select(imp, n_keep):
    n = imp.shape[1]
    col = lax.broadcasted_iota(I32, imp.shape, 1)
    rank = jnp.zeros(imp.shape, I32)
    for j in range(n):
        cj = imp[:, j:j + 1]
        beats = jnp.logical_or(cj > imp, jnp.logical_and(cj == imp, col > j))
        rank = rank + beats.astype(I32)
    return rank < n_keep


def _attn_prompt_kernel(qraw_ref, qrot_ref, ks_ref, vs_ref, kw_ref, vw_ref, kc_ref, vc_ref, gt_ref, ov_ref,
                        ex_ref, o_ref, bias, m_s, l_s, acc_s, *, tq, kc_sel, seq):
    qi = pl.program_id(2)
    q0 = qi * tq
    rows = GQA * tq
    n_sel = ov_ref.shape[1]
    qpos = q0 + lax.broadcasted_iota(I32, (tq, 1), 0)
    qpos8 = q0 + (lax.broadcasted_iota(I32, (rows, 1), 0) & (tq - 1))

    def stack(ref):
        x = ref[...]
        return jnp.concatenate([x[:, g * HEAD_DIM:(g + 1) * HEAD_DIM] for g in range(GQA)], axis=0)

    q_raw = stack(qraw_ref)
    kc = kc_ref[0, 0, 0].astype(BF16)
    vc = vc_ref[0, 0, 0].astype(BF16)
    s = lax.dot_general(q_raw, kc, _NT, preferred_element_type=F32) * ATTN_SCALE
    cmp_end = lax.broadcasted_iota(I32, (1, s.shape[1]), 1) * CMP_STRIDE + (CMP_BLOCK - 1)
    mask = cmp_end <= qpos8
    s = jnp.where(mask, s, NEG_INF)
    p = jnp.where(mask, jnp.exp(s - jnp.max(s, axis=-1, keepdims=True)), 0.0)
    p = p / jnp.maximum(jnp.sum(p, axis=-1, keepdims=True), TINY)
    o_cmp = jnp.dot(p.astype(BF16), vc, preferred_element_type=F32)
    psum = p[0:tq]
    for g in range(1, GQA):
        psum = psum + p[g * tq:(g + 1) * tq]
    p_hi = psum.astype(BF16)
    p_lo = (psum - p_hi.astype(F32)).astype(BF16)
    imp = (jnp.dot(p_hi, ov_ref[...], preferred_element_type=F32)
           + jnp.dot(p_lo, ov_ref[...], preferred_element_type=F32))

    blk = lax.broadcasted_iota(I32, (1, n_sel), 1)
    qblk = qpos // SEL_BLOCK
    valid = blk * SEL_BLOCK <= qpos
    forced = jnp.logical_and(valid, jnp.logical_or(jnp.logical_or(blk == 0, blk == qblk), blk == qblk - 1))
    imp = jnp.where(forced, FORCE_SCORE, jnp.where(valid, imp, NEG_INF))
    sel = jnp.where(_rank_select(imp, TOP_N), 1.0, 0.0).astype(BF16)
    for c in range(seq // kc_sel):
        k0 = c * kc_sel
        open_ = jnp.dot(sel, ex_ref[:, k0:k0 + kc_sel], preferred_element_type=F32) > 0.5
        kpos = k0 + lax.broadcasted_iota(I32, (1, kc_sel), 1)
        bias[:, k0:k0 + kc_sel] = jnp.where(jnp.logical_and(open_, kpos <= qpos), 0.0, NEG_INF)

    q_rot = stack(qrot_ref)

    def init():
        m_s[...] = jnp.full(m_s.shape, NEG_INF, F32)
        l_s[...] = jnp.zeros(l_s.shape, F32)
        acc_s[...] = jnp.zeros(acc_s.shape, F32)

    def update(sc, v):
        reps = sc.shape[1] // HEAD_DIM
        m_prev = m_s[...]
        m_new = jnp.maximum(m_prev, jnp.max(sc, axis=-1, keepdims=True))
        alpha = jnp.exp(m_prev - m_new)
        pr = jnp.exp(sc - jnp.concatenate([m_new] * reps, axis=1))
        l_s[...] = alpha * l_s[...] + jnp.sum(pr, axis=-1, keepdims=True)
        acc_s[...] = alpha * acc_s[...] + jnp.dot(pr.astype(BF16), v, preferred_element_type=F32)
        m_s[...] = m_new

    init()

    def sel_chunk(c, carry):
        k0 = pl.multiple_of(c * kc_sel, kc_sel)
        kch = ks_ref[pl.ds(k0, kc_sel), :].astype(BF16)
        vch = vs_ref[pl.ds(k0, kc_sel), :].astype(BF16)
        sc = lax.dot_general(q_rot, kch, _NT, preferred_element_type=F32) * ATTN_SCALE
        sc = (sc.reshape(GQA, tq, kc_sel) + bias[:, pl.ds(k0, kc_sel)][None]).reshape(rows, kc_sel)
        update(sc, vch)
        return carry

    lax.fori_loop(0, (q0 + tq - 1) // kc_sel + 1, sel_chunk, 0)
    o_sel = acc_s[...] / l_s[...]

    init()
    nwc = jnp.minimum(qi, WINDOW // tq) + 1

    def win_chunk(c, carry):
        k0 = pl.multiple_of((qi - c) * tq, tq)
        kch = kw_ref[pl.ds(k0, tq), :].astype(BF16)
        vch = vw_ref[pl.ds(k0, tq), :].astype(BF16)
        sc = lax.dot_general(q_rot, kch, _NT, preferred_element_type=F32) * ATTN_SCALE
        kp = k0 + lax.broadcasted_iota(I32, (1, tq), 1)
        open_ = jnp.logical_and(kp <= qpos8, kp > qpos8 - WINDOW)
        update(jnp.where(open_, sc, NEG_INF), vch)
        return carry

    lax.fori_loop(0, nwc, win_chunk, 0)
    o_win = acc_s[...] / l_s[...]

    gts = gt_ref[0]
    for g in range(GQA):
        r0, r1 = g * tq, (g + 1) * tq
        c = g * N_BRANCH
        o = (gts[:, c:c + 1] * o_cmp[r0:r1] + gts[:, c + 1:c + 2] * o_sel[r0:r1]
             + gts[:, c + 2:c + 3] * o_win[r0:r1])
        o_ref[:, g * HEAD_DIM:(g + 1) * HEAD_DIM] = o.astype(BF16)


def _attn_prompt(q_raw, q_rot, pg, win, kcvc, gates_k, B, S):
    tq = 128
    kc_sel = 256
    nq = S // tq
    n_cmp = S // CMP_STRIDE - 1
    n_sel = max(-(-S // SEL_BLOCK), TOP_N)
    ov = _overlap(S // CMP_STRIDE, n_sel, n_cmp, n_sel)
    ex = (jnp.arange(n_sel)[:, None] == (jnp.arange(S)[None, :] // SEL_BLOCK)).astype(BF16)
    gcols = SLOT_COLS // HEAD_DIM
    qspec = pl.BlockSpec((tq, GQA * HEAD_DIM), lambda b, k, i: (b * nq + i, k))

    def kvspec(slot):
        return pl.BlockSpec((S, HEAD_DIM), lambda b, k, i: (b, slot * gcols + k))

    def cspec(slot):
        return pl.BlockSpec((1, 1, 1, S // CMP_STRIDE, HEAD_DIM), lambda b, k, i: (b, slot, k, 0, 0))

    rows = GQA * tq
    return pl.pallas_call(
        functools.partial(_attn_prompt_kernel, tq=tq, kc_sel=kc_sel, seq=S),
        out_shape=jax.ShapeDtypeStruct((B * S, Q_COLS), BF16),
        grid=(B, N_KV_HEADS, nq),
        in_specs=[
            qspec, qspec, kvspec(2), kvspec(3), kvspec(0), kvspec(1), cspec(0), cspec(1),
            pl.BlockSpec((1, tq, HEAD_DIM), lambda b, k, i: (k, b * nq + i, 0)),
            pl.BlockSpec(ov.shape, lambda b, k, i: (0, 0)),
            pl.BlockSpec(ex.shape, lambda b, k, i: (0, 0)),
        ],
        out_specs=qspec,
        scratch_shapes=[pltpu.VMEM((tq, S), F32), pltpu.VMEM((rows, HEAD_DIM), F32),
                        pltpu.VMEM((rows, HEAD_DIM), F32), pltpu.VMEM((rows, HEAD_DIM), F32)],
        compiler_params=_params(("arbitrary", "arbitrary", "arbitrary")),
    )(q_raw, q_rot, pg, pg, win, win, kcvc, kcvc, gates_k, ov, ex)


N_SEL_SAMPLE = -(-(PAST_LEN + 1) // SEL_BLOCK)
SEL_LANES = -(-N_SEL_SAMPLE // 128) * 128
NEW_BLOCK = PAST_LEN // SEL_BLOCK


def _sample_select_kernel(q_ref, kc_ref, vc_ref, ov_ref, ocmp_ref, sel_ref):
    n_cmp_rows = kc_ref.shape[3]
    for kh in range(N_KV_HEADS):
        q = q_ref[0, kh * GQA:(kh + 1) * GQA, :].astype(BF16)
        kc = kc_ref[0, 0, kh].astype(BF16)
        vc = vc_ref[0, 0, kh].astype(BF16)
        s = lax.dot_general(q, kc, _NT, preferred_element_type=F32) * ATTN_SCALE
        cmp_end = lax.broadcasted_iota(I32, (1, n_cmp_rows), 1) * CMP_STRIDE + (CMP_BLOCK - 1)
        mask = cmp_end <= PAST_LEN
        s = jnp.where(mask, s, NEG_INF)
        p = jnp.where(mask, jnp.exp(s - jnp.max(s, axis=-1, keepdims=True)), 0.0)
        p = p / jnp.maximum(jnp.sum(p, axis=-1, keepdims=True), TINY)
        ocmp_ref[0, kh * GQA:(kh + 1) * GQA, :] = jnp.dot(p.astype(BF16), vc, preferred_element_type=F32)
        psum = jnp.broadcast_to(jnp.sum(p, axis=0, keepdims=True), (128, n_cmp_rows))
        p_hi = psum.astype(BF16)
        p_lo = (psum - p_hi.astype(F32)).astype(BF16)
        imp = (jnp.dot(p_hi, ov_ref[...], preferred_element_type=F32)
               + jnp.dot(p_lo, ov_ref[...], preferred_element_type=F32))
        blk = lax.broadcasted_iota(I32, imp.shape, 1)
        valid = blk * SEL_BLOCK <= PAST_LEN
        forced = jnp.logical_and(valid, jnp.logical_or(jnp.logical_or(blk == 0, blk == NEW_BLOCK),
                                                       blk == NEW_BLOCK - 1))
        imp = jnp.where(forced, FORCE_SCORE, jnp.where(valid, imp, NEG_INF))
        col = imp.T[:, 0:1]
        row = imp[0:1, :]
        jj = lax.broadcasted_iota(I32, (SEL_LANES, SEL_LANES), 0)
        ii = lax.broadcasted_iota(I32, (SEL_LANES, SEL_LANES), 1)
        beats = jnp.logical_or(col > row, jnp.logical_and(col == row, jj < ii))
        rank = jnp.sum(beats.astype(I32), axis=0, keepdims=True)
        lane = lax.broadcasted_iota(I32, (1, 128), 1)
        out = jnp.zeros((1, 128), I32)
        for r in range(TOP_N):
            idx = jnp.sum(jnp.where(rank == r, blk[0:1, :], 0), axis=-1, keepdims=True)
            out = jnp.where(lane == r, idx, out)
        sel_ref[0, kh:kh + 1, :] = out


def _sample_select(q_raw3, kcvc, nb):
    n_cmp_rows = kcvc.shape[3]
    ov = _overlap(n_cmp_rows, SEL_LANES, n_cmp_rows - 1, N_SEL_SAMPLE)

    def cspec(slot):
        return pl.BlockSpec((1, 1, N_KV_HEADS, n_cmp_rows, HEAD_DIM), lambda b: (b, slot, 0, 0, 0))

    return pl.pallas_call(
        _sample_select_kernel,
        out_shape=(jax.ShapeDtypeStruct((nb, N_HEADS, HEAD_DIM), F32),
                   jax.ShapeDtypeStruct((nb, N_KV_HEADS, 128), I32)),
        grid=(nb,),
        in_specs=[pl.BlockSpec((1, N_HEADS, HEAD_DIM), lambda b: (b, 0, 0)), cspec(0), cspec(1),
                  pl.BlockSpec(ov.shape, lambda b: (0, 0))],
        out_specs=(pl.BlockSpec((1, N_HEADS, HEAD_DIM), lambda b: (b, 0, 0)),
                   pl.BlockSpec((1, N_KV_HEADS, 128), lambda b: (b, 0, 0))),
        compiler_params=_params(("arbitrary",)),
    )(q_raw3, kcvc, kcvc, ov)


def _sample_attend_kernel(sel, pt, q_ref, ocmp_ref, pgn_ref, winn_ref, st_ref, gt_ref, cache_hbm, o_ref,
                          kbuf, vbuf, sem, *, npg):
    b = pl.program_id(0)

    def block_copies(kh, r):
        j = jnp.minimum(sel[(b * N_KV_HEADS + kh) * 128 + r], NEW_BLOCK - 1)
        row0 = pt[b * npg + j // 2] * PAGE_SIZE + (j % 2) * SEL_BLOCK
        dst = pl.ds(r * SEL_BLOCK, SEL_BLOCK)
        return (
            pltpu.make_async_copy(
                cache_hbm.at[pl.ds(row0, SEL_BLOCK), pl.ds((2 * N_KV_HEADS + kh) * HEAD_DIM, HEAD_DIM)],
                kbuf.at[kh, dst], sem),
            pltpu.make_async_copy(
                cache_hbm.at[pl.ds(row0, SEL_BLOCK), pl.ds((3 * N_KV_HEADS + kh) * HEAD_DIM, HEAD_DIM)],
                vbuf.at[kh, dst], sem),
        )

    for kh in range(N_KV_HEADS):
        for r in range(TOP_N):
            for cp in block_copies(kh, r):
                cp.start()
    for kh in range(N_KV_HEADS):
        for r in range(TOP_N):
            for cp in block_copies(kh, r):
                cp.wait()

    n_keys = TOP_N * SEL_BLOCK
    lane_blk = lax.broadcasted_iota(I32, (1, n_keys), 1) // SEL_BLOCK
    lane_off = lax.broadcasted_iota(I32, (1, n_keys), 1) % SEL_BLOCK
    wb = st_ref.shape[1]
    for kh in range(N_KV_HEADS):
        q = q_ref[0, kh * GQA:(kh + 1) * GQA, :].astype(BF16)
        qf = q.astype(F32)

        def attend(sc, open_, v, k_new, v_new, new_open):
            s_new = jnp.sum(qf * k_new.astype(BF16).astype(F32), axis=-1, keepdims=True) * ATTN_SCALE
            s_new = jnp.where(new_open, s_new, NEG_INF)
            sc = jnp.where(open_, sc, NEG_INF)
            m = jnp.maximum(jnp.max(sc, axis=-1, keepdims=True), s_new)
            pr = jnp.where(open_, jnp.exp(sc - m), 0.0)
            p_new = jnp.where(new_open, jnp.exp(s_new - m), 0.0)
            den = jnp.maximum(jnp.sum(pr, axis=-1, keepdims=True) + p_new, TINY)
            pr = pr / den
            p_new = p_new / den
            return (jnp.dot(pr.astype(BF16), v, preferred_element_type=F32)
                    + p_new.astype(BF16).astype(F32) * v_new.astype(BF16).astype(F32))

        blk_id = jnp.zeros((1, n_keys), I32)
        has_new = jnp.zeros((1, 1), I32)
        for r in range(TOP_N):
            j = sel[(b * N_KV_HEADS + kh) * 128 + r]
            blk_id = jnp.where(lane_blk == r, j, blk_id)
            has_new = jnp.where(j == NEW_BLOCK, 1, has_new)
        kpos = blk_id * SEL_BLOCK + lane_off
        sc = lax.dot_general(q, kbuf[kh].astype(BF16), _NT, preferred_element_type=F32) * ATTN_SCALE
        c_k = (2 * N_KV_HEADS + kh) * HEAD_DIM
        c_v = (3 * N_KV_HEADS + kh) * HEAD_DIM
        o_sel = attend(sc, kpos < PAST_LEN, vbuf[kh].astype(BF16), pgn_ref[0, :, c_k:c_k + HEAD_DIM],
                       pgn_ref[0, :, c_v:c_v + HEAD_DIM], has_new > 0)

        c_k = kh * HEAD_DIM
        c_v = (N_KV_HEADS + kh) * HEAD_DIM
        kw = st_ref[0, :, c_k:c_k + HEAD_DIM].astype(BF16)
        vw = st_ref[0, :, c_v:c_v + HEAD_DIM].astype(BF16)
        sc = lax.dot_general(q, kw, _NT, preferred_element_type=F32) * ATTN_SCALE
        kp = PAST_LEN - wb + lax.broadcasted_iota(I32, (1, wb), 1)
        open_ = jnp.logical_and(kp > PAST_LEN - WINDOW, kp >= 0)
        o_win = attend(sc, open_, vw, winn_ref[0, :, c_k:c_k + HEAD_DIM], winn_ref[0, :, c_v:c_v + HEAD_DIM],
                       jnp.ones((1, 1), I32) > 0)

        gts = gt_ref[0, kh]
        o_ref[0, kh * GQA:(kh + 1) * GQA, :] = (
            gts[:, 0:1] * ocmp_ref[0, kh * GQA:(kh + 1) * GQA, :] + gts[:, 1:2] * o_sel + gts[:, 2:3] * o_win)


def _sample_attend(sel, page_table, q_rot3, o_cmp, pg_new, win_new, state_win2, gates4, cache2d):
    nb, npg = page_table.shape
    wb = state_win2.shape[1]
    n_keys = TOP_N * SEL_BLOCK
    hspec = pl.BlockSpec((1, N_HEADS, HEAD_DIM), lambda b, s, p: (b, 0, 0))
    grid_spec = pltpu.PrefetchScalarGridSpec(
        num_scalar_prefetch=2,
        grid=(nb,),
        in_specs=[
            hspec, hspec,
            pl.BlockSpec((1, 1, PG_COLS), lambda b, s, p: (b, 0, 0)),
            pl.BlockSpec((1, 1, WIN_COLS), lambda b, s, p: (b, 0, 0)),
            pl.BlockSpec((1, wb, WIN_COLS), lambda b, s, p: (b, 0, 0)),
            pl.BlockSpec((1, N_KV_HEADS, GQA, 128), lambda b, s, p: (b, 0, 0, 0)),
            pl.BlockSpec(memory_space=pl.ANY),
        ],
        out_specs=hspec,
        scratch_shapes=[pltpu.VMEM((N_KV_HEADS, n_keys, HEAD_DIM), F32),
                        pltpu.VMEM((N_KV_HEADS, n_keys, HEAD_DIM), F32), pltpu.SemaphoreType.DMA(())],
    )
    return pl.pallas_call(
        functools.partial(_sample_attend_kernel, npg=npg),
        out_shape=jax.ShapeDtypeStruct((nb, N_HEADS, HEAD_DIM), F32),
        grid_spec=grid_spec,
        compiler_params=_params(("arbitrary",)),
    )(sel.reshape(-1), page_table.reshape(-1), q_rot3, o_cmp, pg_new, win_new, state_win2, gates4, cache2d)


def _rope_tables(pos):
    half = HEAD_DIM // 2
    inv = jnp.power(ROPE_THETA, -jnp.arange(half, dtype=F32) / half)
    ang = pos.astype(F32)[:, None] * inv[None, :]
    cos, sin = jnp.cos(ang), jnp.sin(ang)
    return jnp.concatenate([cos, cos], axis=1), jnp.concatenate([-sin, sin], axis=1)


def _nsa_weights(w_in, w_o, qk_g, pe, w1, w2):
    w_q = w_in[:, :Q_COLS].astype(BF16)
    w_kvg = jnp.pad(w_in[:, Q_COLS:], ((0, 0), (0, SLOT_COLS - GATE_COLS))).astype(BF16)
    return dict(w_q=w_q, w_kvg=w_kvg, w_o=w_o.astype(BF16), qg_q=qk_g[0:1], qg_kv=qk_g[2:4], gk=qk_g[1:2],
                pe=pe, w1=w1.astype(BF16), w2=w2.astype(BF16))


def _nsa_prompt(x, shift, scale, gate, g, nw):
    B, S, D = x.shape
    cosf, sinf = _rope_tables(jnp.arange(S))
    tm = 512
    q_raw, q_rot = _proj_q(x, shift, scale, g, nw["w_q"], cosf, sinf, nw["qg_q"], tm)
    pg, win, gates = _proj_kv(x, shift, scale, g, nw["w_kvg"], cosf, sinf, nw["qg_kv"], tm)
    pages = jnp.arange(B * (S // PAGE_SIZE), dtype=I32)
    lead, tail = _cmp_stage(pg, pages, B, nw["pe"], nw["w1"])
    kcvc = _cmp_finish(lead, tail, nw["w2"], nw["gk"])
    gates_k = gates[:, :GATE_COLS].reshape(B * S, N_KV_HEADS, GQA * N_BRANCH).transpose(1, 0, 2)
    gates_k = jnp.pad(gates_k, ((0, 0), (0, 0), (0, HEAD_DIM - GQA * N_BRANCH)))
    o = _attn_prompt(q_raw, q_rot, pg, win, kcvc, gates_k, B, S)
    y = _out_proj(o, nw["w_o"], x, gate, 1024)
    return y, pg, win


def _nsa_sample(x, shift, scale, gate, g, nw, cache_kv_j, state_win_j, page_table, nb):
    R = x.shape[1]
    cosf, sinf = _rope_tables(jnp.full((R,), PAST_LEN))
    q_raw, q_rot = _proj_q(x, shift, scale, g, nw["w_q"], cosf, sinf, nw["qg_q"], R)
    pg, win, gates = _proj_kv(x, shift, scale, g, nw["w_kvg"], cosf, sinf, nw["qg_kv"], R)
    n_phys = cache_kv_j.shape[0]
    cache2d = cache_kv_j.reshape(n_phys * PAGE_SIZE, PG_COLS)
    lead, tail = _cmp_stage(cache2d, page_table.reshape(-1), nb, nw["pe"], nw["w1"])
    kcvc = _cmp_finish(lead, tail, nw["w2"], nw["gk"])
    q_raw3 = q_raw[:nb].reshape(nb, N_HEADS, HEAD_DIM).astype(F32)
    q_rot3 = q_rot[:nb].reshape(nb, N_HEADS, HEAD_DIM).astype(F32)
    o_cmp, sel = _sample_select(q_raw3, kcvc, nb)
    gates4 = gates[:nb, :GATE_COLS].reshape(nb, N_KV_HEADS, GQA, N_BRANCH)
    gates4 = jnp.pad(gates4, ((0, 0), (0, 0), (0, 0), (0, 128 - N_BRANCH)))
    wb = state_win_j.shape[1]
    o3 = _sample_attend(sel, page_table, q_rot3, o_cmp, pg[:nb, None, :], win[:nb, None, :],
                        state_win_j.reshape(nb, wb, WIN_COLS), gates4, cache2d)
    o = jnp.pad(o3.reshape(nb, Q_COLS), ((0, R - nb), (0, 0))).astype(BF16)
    y = _out_proj(o, nw["w_o"], x, gate, R)
    return y, pg[:nb], win[:nb]


def kernel(x_prompt, x_sample, cache_kv, state_win, state_pool, page_table, c_prompt, c_sample, ada_w, ada_b,
           norm_g, pool_w, pool_scale, nsa_w_in, nsa_w_o, nsa_qk_g, nsa_cmp_pe, nsa_cmp_w1, nsa_cmp_w2, router_w,
           moe_w1, moe_w3, moe_w2):
    B, S, D = x_prompt.shape
    nb = x_sample.shape[0]
    R = SAMPLE_ROWS
    depth = ada_w.shape[0]
    wb = state_win.shape[2]

    c_all = jnp.concatenate([jnp.pad(c_prompt, ((0, R - B), (0, 0))), jnp.pad(c_sample, ((0, R - nb), (0, 0)))])
    mods = _ada(c_all, ada_w, ada_b).reshape(depth, 2 * R, 6, D)

    def mod_p(i, k):
        return mods[i, :B, k][:, None, :]

    def mod_s(i, k):
        return mods[i, R:, k][None]

    rw_t = router_w.T
    rw_hi = rw_t.astype(BF16)
    rw_lo = (rw_t - rw_hi.astype(F32)).astype(BF16)
    pool_w_bf = pool_w.astype(BF16)

    xp = x_prompt
    xs = jnp.pad(x_sample.reshape(nb, D), ((0, R - nb), (0, 0)))[None]
    outs = {}
    for i in range(depth):
        j = i // 2
        g1, g2 = norm_g[i, 0:1], norm_g[i, 1:2]
        if i % 2 == 0:
            xp, tail = _pool_prompt(xp, mod_p(i, 0), mod_p(i, 1), mod_p(i, 2), g1, pool_w_bf[j], pool_scale[j:j + 1])
            outs.setdefault("pool_p", []).append(tail[:, POOL_HALO - POOL_BUF:])
            st = jnp.pad(state_pool[j], ((0, R - nb), (0, 0), (0, 0))).transpose(1, 0, 2)
            xs2, hs = _pool_sample(xs[0], st, mod_s(i, 0)[0], mod_s(i, 1)[0], mod_s(i, 2)[0], g1, pool_w_bf[j],
                                   pool_scale[j:j + 1])
            xs = xs2[None]
            outs.setdefault("pool_s", []).append(jnp.concatenate([state_pool[j][:, 1:], hs[:nb, None, :]], axis=1))
        else:
            nw = _nsa_weights(nsa_w_in[j], nsa_w_o[j], nsa_qk_g[j], nsa_cmp_pe[j], nsa_cmp_w1[j], nsa_cmp_w2[j])
            xp, pg, win = _nsa_prompt(xp, mod_p(i, 0), mod_p(i, 1), mod_p(i, 2), g1, nw)
            outs.setdefault("kv_p", []).append(
                pg.reshape(B, S // PAGE_SIZE, PAGE_SIZE, 4, N_KV_HEADS, HEAD_DIM))
            win_p = win.reshape(B, S, 2, N_KV_HEADS, HEAD_DIM)[:, max(S - wb, 0):]
            if S < wb:
                win_p = jnp.pad(win_p, ((0, 0), (wb - S, 0), (0, 0), (0, 0), (0, 0)))
            outs.setdefault("win_p", []).append(win_p)
            xs, pg_s, win_s = _nsa_sample(xs, mod_s(i, 0), mod_s(i, 1), mod_s(i, 2), g1, nw, cache_kv[j],
                                          state_win[j], page_table, nb)
            outs.setdefault("kv_s", []).append(pg_s.reshape(nb, 1, 4, N_KV_HEADS, HEAD_DIM))
            outs.setdefault("win_s", []).append(jnp.concatenate(
                [state_win[j][:, 1:], win_s.reshape(nb, 1, 2, N_KV_HEADS, HEAD_DIM)], axis=1))
        xp = _moe_block(xp, mod_p(i, 3), mod_p(i, 4), mod_p(i, 5), g2, rw_hi, rw_lo, moe_w1, moe_w3, moe_w2, i,
                        n_rows=B * S, tm_r=256, tm_e=512, ch=256, tc=128)
        xs = _moe_block(xs, mod_s(i, 3), mod_s(i, 4), mod_s(i, 5), g2, rw_hi, rw_lo, moe_w1, moe_w3, moe_w2, i,
                        n_rows=nb, tm_r=R, tm_e=16, ch=16, tc=R)
    return (xp, xs[0, :nb].reshape(nb, 1, D), jnp.stack(outs["kv_p"]), jnp.stack(outs["kv_s"]),
            jnp.stack(outs["win_p"]), jnp.stack(outs["win_s"]), jnp.stack(outs["pool_p"]),
            jnp.stack(outs["pool_s"]))
```
